```python
import math
import jax, jax.numpy as jnp
from jax import lax
import numpy as np

D_MODEL = 2048
BATCH = 2
SEQ = 4096
DEPTH = 2

GRID_W = 64
CTX_LEN = 256
RMS_EPS = 1e-6
N_MOD = 6

DA_HEADS = 8
DA_HEAD_DIM = 64
DA_QK_WIDTH = DA_HEADS * 2 * DA_HEAD_DIM
DA_V_WIDTH = DA_HEADS * 2 * DA_HEAD_DIM
Q_BLOCK = 128
ROPE_BASE = 10000.0

LRU_WIDTH = 1024
LRU_BLOCKS = 8
LRU_BLOCK = LRU_WIDTH // LRU_BLOCKS
LRU_C = 8.0
CONV_W = 4

SSD_HEADS = 16
SSD_HEAD_DIM = 64
SSD_INNER = SSD_HEADS * SSD_HEAD_DIM
SSD_STATE = 64
SSD_GROUPS = 4
SSD_CONV_DIM = SSD_INNER + 2 * SSD_GROUPS * SSD_STATE
SSD_CHUNK = 128

N_BRANCH = 3
BRANCH_WIDTH = 1024

PEER_HEADS = 8
PEER_KEYS = 128
PEER_N = PEER_KEYS * PEER_KEYS
PEER_DK = 256
PEER_TOPK = 16
PEER_BLOCK = 128

IN_SPLITS = (DA_QK_WIDTH, DA_QK_WIDTH, DA_V_WIDTH, LRU_WIDTH, LRU_WIDTH, SSD_INNER, SSD_CONV_DIM, 2 * SSD_HEADS, N_BRANCH * D_MODEL)
IN_WIDTH = sum(IN_SPLITS)

kernel_name = 'hybrid_dit_diffattn_rglru_ssd_peer'


def rmsnorm(x, g):
    xf = x.astype(jnp.float32)
    y = xf * lax.rsqrt(jnp.mean(xf * xf, axis=-1, keepdims=True) + RMS_EPS)
    return y.astype(x.dtype) * g.astype(x.dtype)


def split_in(u):
    offs = []
    acc = 0
    for w in IN_SPLITS[:-1]:
        acc += w
        offs.append(acc)
    return jnp.split(u, offs, axis=-1)


def dwconv_centred(x, w, b):
    ch = x.shape[-1]
    left = CONV_W // 2
    y = lax.conv_general_dilated(x, w[:, None, :].astype(x.dtype), window_strides=(1,),
                                 padding=[(left, CONV_W - 1 - left)],
                                 dimension_numbers=('NWC', 'WIO', 'NWC'), feature_group_count=ch)
    return y + b.astype(x.dtype)


def axial_rope(rows, dtype):
    row = jnp.repeat(jnp.arange(rows), GRID_W).astype(jnp.float32)
    col = jnp.tile(jnp.arange(GRID_W), rows).astype(jnp.float32)
    n_freq = DA_HEAD_DIM // 4
    inv = jnp.power(ROPE_BASE, -jnp.arange(n_freq, dtype=jnp.float32) / n_freq)
    ang = jnp.concatenate([row[:, None] * inv, col[:, None] * inv], axis=-1)
    return jnp.cos(ang).astype(dtype), jnp.sin(ang).astype(dtype)


def apply_rope(t, cos, sin):
    half = t.shape[-1] // 2
    t1, t2 = t[..., :half], t[..., half:]
    cs = cos[:, None, None, :]
    sn = sin[:, None, None, :]
    return jnp.concatenate([t1 * cs - t2 * sn, t1 * sn + t2 * cs], axis=-1)


def diff_softmax(q, k, v, lam):
    s = jnp.einsum('bqhcd,bkhcd->bhcqk', q, k).astype(jnp.float32) * (DA_HEAD_DIM ** -0.5)
    p = jax.nn.softmax(s, axis=-1)
    a = p[:, :, 0] - lam * p[:, :, 1]
    return jnp.einsum('bhqk,bkhe->bqhe', a.astype(v.dtype), v)


def diff_attention(q_lat, k_lat, v_lat, q_ctx, k_ctx, v_ctx, cos, sin, da_lambda, subln_g, layer, need_ctx):
    bsz, t, _ = q_lat.shape
    heads = lambda a: a.reshape(a.shape[0], a.shape[1], DA_HEADS, 2, DA_HEAD_DIM)
    vheads = lambda a: a.reshape(a.shape[0], a.shape[1], DA_HEADS, 2 * DA_HEAD_DIM)
    ql = apply_rope(heads(q_lat), cos, sin)
    kl = apply_rope(heads(k_lat), cos, sin)
    kc = heads(k_ctx)
    vc = vheads(v_ctx)
    lam_init = 0.8 - 0.6 * math.exp(-0.3 * layer)
    lp = da_lambda.astype(jnp.float32)
    lam = jnp.exp(jnp.sum(lp[0] * lp[1])) - jnp.exp(jnp.sum(lp[2] * lp[3])) + lam_init
    k_all = jnp.concatenate([kl, kc], axis=1)
    v_all = jnp.concatenate([vheads(v_lat), vc], axis=1)
    nb = t // Q_BLOCK
    qb = ql.reshape(bsz, nb, Q_BLOCK, DA_HEADS, 2, DA_HEAD_DIM).swapaxes(0, 1)
    o = lax.map(lambda qx: diff_softmax(qx, k_all, v_all, lam), qb)
    o = o.swapaxes(0, 1).reshape(bsz, t, DA_HEADS, 2 * DA_HEAD_DIM)
    finish = lambda y: (rmsnorm(y, subln_g) * (1.0 - lam_init)).reshape(y.shape[0], y.shape[1], DA_V_WIDTH)
    o_lat = finish(o)
    o_ctx = finish(diff_softmax(heads(q_ctx), kc, vc, lam)) if need_ctx else None
    return o_lat, o_ctx


def lin_combine(e1, e2):
    a1, b1 = e1
    a2, b2 = e2
    return a1 * a2, a2 * b1 + b2


def linear_scan(a, b, h0):
    if h0 is not None:
        b = b.at[:, 0].add(a[:, 0] * h0)
    return lax.associative_scan(lin_combine, (a, b), axis=1)[1]


def rglru_coeffs(xx, gate_w, gate_b, lam):
    xf = xx.astype(jnp.float32)
    bsz, t, _ = xf.shape
    xb = xf.reshape(bsz, t, LRU_BLOCKS, LRU_BLOCK)
    gates = jnp.einsum('btgi,kgij->btkgj', xb, gate_w.astype(jnp.float32)).reshape(bsz, t, 2, LRU_WIDTH)
    gates = gates + gate_b.astype(jnp.float32)
    r = jax.nn.sigmoid(gates[:, :, 0])
    i = jax.nn.sigmoid(gates[:, :, 1])
    log_a = -LRU_C * r * jax.nn.softplus(-lam.astype(jnp.float32))
    a = jnp.exp(log_a)
    beta = jnp.sqrt(-jnp.expm1(2.0 * log_a))
    return a, beta * (i * xf)


def rglru_branch(x_lat, y_lat, x_ctx, y_ctx, conv_w, conv_b, gate_w, gate_b, lam, need_ctx):
    flip = lambda a: jnp.flip(a, axis=1)
    xl = dwconv_centred(x_lat, conv_w, conv_b)
    xc = dwconv_centred(x_ctx, conv_w, conv_b)
    a, bb = rglru_coeffs(xc, gate_w[0], gate_b[0], lam[0])
    hcf = linear_scan(a, bb, None)
    a, bb = rglru_coeffs(flip(xc), gate_w[1], gate_b[1], lam[1])
    hcb = linear_scan(a, bb, None)
    a, bb = rglru_coeffs(xl, gate_w[0], gate_b[0], lam[0])
    hlf = linear_scan(a, bb, hcf[:, -1])
    a, bb = rglru_coeffs(flip(xl), gate_w[1], gate_b[1], lam[1])
    hlb = linear_scan(a, bb, hcb[:, -1])
    out_lat = (hlf + flip(hlb)).astype(x_lat.dtype) * jax.nn.gelu(y_lat)
    out_ctx = (hcf + flip(hcb)).astype(x_ctx.dtype) * jax.nn.gelu(y_ctx) if need_ctx else None
    return out_lat, out_ctx


def ssd_chunked(x, dt, A, Bg, Cg, h0):
    bsz, t, nh, p = x.shape
    L = SSD_CHUNK
    nc = t // L
    rep = nh // Bg.shape[2]
    Bh = jnp.repeat(Bg, rep, axis=2).reshape(bsz, nc, L, nh, SSD_STATE)
    Ch = jnp.repeat(Cg, rep, axis=2).reshape(bsz, nc, L, nh, SSD_STATE)
    x = x.reshape(bsz, nc, L, nh, p)
    dt = dt.reshape(bsz, nc, L, nh)
    acum = jnp.cumsum(dt * A, axis=2)
    seg = acum[:, :, :, None, :] - acum[:, :, None, :, :]
    causal = jnp.tril(jnp.ones((L, L), dtype=bool))[None, None, :, :, None]
    lmat = jnp.exp(jnp.where(causal, seg, -jnp.inf))
    scores = jnp.einsum('bclhn,bcshn->bclsh', Ch, Bh) * lmat
    y_diag = jnp.einsum('bclsh,bcsh,bcshp->bclhp', scores, dt, x)
    decay_to_end = jnp.exp(acum[:, :, -1:, :] - acum)
    states = jnp.einsum('bclhn,bclh,bclhp->bchpn', Bh, decay_to_end * dt, x)
    chunk_decay = jnp.exp(acum[:, :, -1, :])

    def step(h, inp):
        dec, st = inp
        return dec[:, :, None, None] * h + st, h

    h_last, h_start = lax.scan(step, h0, (chunk_decay.swapaxes(0, 1), states.swapaxes(0, 1)))
    h_start = h_start.swapaxes(0, 1)
    y_off = jnp.einsum('bclhn,bchpn,bclh->bclhp', Ch, h_start, jnp.exp(acum))
    return (y_diag + y_off).reshape(bsz, t, nh, p), h_last


def ssd_prep(xbc, dt, conv_w, conv_b, dt_bias):
    u = jax.nn.silu(dwconv_centred(xbc, conv_w, conv_b)).astype(jnp.float32)
    bsz, t, _ = u.shape
    xs, bs, cs = jnp.split(u, [SSD_INNER, SSD_INNER + SSD_GROUPS * SSD_STATE], axis=-1)
    xh = xs.reshape(bsz, t, SSD_HEADS, SSD_HEAD_DIM)
    bg = bs.reshape(bsz, t, SSD_GROUPS, SSD_STATE)
    cg = cs.reshape(bsz, t, SSD_GROUPS, SSD_STATE)
    dtf = jax.nn.softplus(dt.astype(jnp.float32).reshape(bsz, t, 2, SSD_HEADS) + dt_bias.astype(jnp.float32))
    return xh, bg, cg, dtf


def ssd_branch(z_lat, xbc_lat, dt_lat, z_ctx, xbc_ctx, dt_ctx, conv_w, conv_b, dt_bias, a_log, d_skip, norm_g, need_ctx):
    flip = lambda a: jnp.flip(a, axis=1)
    A = -jnp.exp(a_log.astype(jnp.float32))
    xc, bc, cc, dtc = ssd_prep(xbc_ctx, dt_ctx, conv_w, conv_b, dt_bias)
    xl, bl, cl, dtl = ssd_prep(xbc_lat, dt_lat, conv_w, conv_b, dt_bias)
    h0 = jnp.zeros((xc.shape[0], SSD_HEADS, SSD_HEAD_DIM, SSD_STATE), jnp.float32)
    ycf, scf = ssd_chunked(xc, dtc[:, :, 0], A[0], bc, cc, h0)
    ycb, scb = ssd_chunked(flip(xc), flip(dtc[:, :, 1]), A[1], flip(bc), flip(cc), h0)
    ylf, _ = ssd_chunked(xl, dtl[:, :, 0], A[0], bl, cl, scf)
    ylb, _ = ssd_chunked(flip(xl), flip(dtl[:, :, 1]), A[1], flip(bl), flip(cl), scb)
    dsk = d_skip.astype(jnp.float32)[:, None]

    def finish(yf, yb, xh, z):
        y = (yf + flip(yb) + dsk * xh).reshape(z.shape[0], z.shape[1], SSD_INNER)
        return rmsnorm(y * jax.nn.silu(z.astype(jnp.float32)), norm_g).astype(z.dtype)

    out_lat = finish(ylf, ylb, xl, z_lat)
    out_ctx = finish(ycf, ycb, xc, z_ctx) if need_ctx else None
    return out_lat, out_ctx


def merge_branches(att, lru, ssd, gates, w_branch, w_out):
    g = jax.nn.sigmoid(gates.astype(jnp.float32)).astype(att.dtype)
    g = g.reshape(g.shape[0], g.shape[1], N_BRANCH, D_MODEL)
    m = g[:, :, 0] * (att @ w_branch[0]) + g[:, :, 1] * (lru @ w_branch[1]) + g[:, :, 2] * (ssd @ w_branch[2])
    return m @ w_out


def token_mixer(h_lat, h_ctx, cos, sin, layer, need_ctx, w_in, da_lambda, da_subln_g,
                lru_conv_w, lru_conv_b, lru_gate_w, lru_gate_b, lru_lambda,
                ssd_conv_w, ssd_conv_b, ssd_dt_bias, ssd_a_log, ssd_d, ssd_norm_g, w_branch, w_out):
    ql, kl, vl, lxl, lyl, zl, xbcl, dtl, gl = split_in(h_lat @ w_in)
    qc, kc, vc, lxc, lyc, zc, xbcc, dtc, gc = split_in(h_ctx @ w_in)
    att_l, att_c = diff_attention(ql, kl, vl, qc, kc, vc, cos, sin, da_lambda, da_subln_g, layer, need_ctx)
    lru_l, lru_c = rglru_branch(lxl, lyl, lxc, lyc, lru_conv_w, lru_conv_b, lru_gate_w, lru_gate_b, lru_lambda, need_ctx)
    ssd_l, ssd_c = ssd_branch(zl, xbcl, dtl, zc, xbcc, dtc, ssd_conv_w, ssd_conv_b, ssd_dt_bias, ssd_a_log, ssd_d, ssd_norm_g, need_ctx)
    y_lat = merge_branches(att_l, lru_l, ssd_l, gl, w_branch, w_out)
    y_ctx = merge_branches(att_c, lru_c, ssd_c, gc, w_branch, w_out) if need_ctx else None
    return y_lat, y_ctx


def peer_ffn(h, wq, keys, u_tab, v_tab):
    bsz, t, d = h.shape
    q = (h @ wq).reshape(bsz, t, PEER_HEADS, 2, PEER_DK // 2)
    s = jnp.einsum('bthcd,hckd->bthck', q, keys).astype(jnp.float32)
    top_s, top_i = lax.top_k(s, PEER_TOPK)
    cand_s = (top_s[..., 0, :, None] + top_s[..., 1, None, :]).reshape(bsz, t, PEER_HEADS, PEER_TOPK * PEER_TOPK)
    cand_i = (top_i[..., 0, :, None] * PEER_KEYS + top_i[..., 1, None, :]).reshape(bsz, t, PEER_HEADS, PEER_TOPK * PEER_TOPK)
    best_s, best_pos = lax.top_k(cand_s, PEER_TOPK)
    expert = jnp.take_along_axis(cand_i, best_pos, axis=-1)
    gate = jax.nn.softmax(best_s, axis=-1).astype(h.dtype)
    n_blk = (bsz * t) // PEER_BLOCK
    hb = h.reshape(n_blk, PEER_BLOCK, d)
    eb = expert.reshape(n_blk, PEER_BLOCK, PEER_HEADS * PEER_TOPK)
    gb = gate.reshape(n_blk, PEER_BLOCK, PEER_HEADS * PEER_TOPK)

    def block(args):
        hx, ex, gx = args
        act = jax.nn.gelu(jnp.einsum('nd,ned->ne', hx, u_tab[ex])) * gx
        return jnp.einsum('ne,ned->nd', act, v_tab[ex])

    return lax.map(block, (hb, eb, gb)).reshape(bsz, t, d)


def setup_inputs(seed: int = 0) -> dict:
    key = jax.random.key(seed)
    ks = jax.random.split(key, 32)
    f32 = jnp.float32
    D = D_MODEL
    nrm = lambda k, shape, scale: jax.random.normal(k, shape, f32) * scale
    dt0 = jnp.exp(jax.random.uniform(ks[17], (DEPTH, 2, SSD_HEADS), f32, math.log(1e-3), math.log(1e-1)))
    a0 = jax.random.uniform(ks[14], (DEPTH, 2, LRU_WIDTH), f32, 0.9, 0.999)
    return {
        'x': nrm(ks[0], (BATCH, SEQ, D), 1.0),
        'c': nrm(ks[1], (BATCH, D), 1.0),
        'ctx': nrm(ks[2], (BATCH, CTX_LEN, D), 1.0),
        'c_ctx': nrm(ks[3], (D,), 1.0),
        'ada_w': nrm(ks[4], (DEPTH, D, N_MOD * D), 0.5 * D ** -0.5),
        'ada_b': nrm(ks[5], (DEPTH, N_MOD * D), 0.02),
        'norm1_g': 1.0 + nrm(ks[6], (DEPTH, D), 0.02),
        'norm2_g': 1.0 + nrm(ks[7], (DEPTH, D), 0.02),
        'w_in': nrm(ks[8], (DEPTH, D, IN_WIDTH), D ** -0.5),
        'da_lambda': nrm(ks[9], (DEPTH, 4, DA_HEAD_DIM), 0.1),
        'da_subln_g': 1.0 + nrm(ks[10], (DEPTH, 2 * DA_HEAD_DIM), 0.02),
        'lru_conv_w': nrm(ks[11], (DEPTH, CONV_W, LRU_WIDTH), CONV_W ** -0.5),
        'lru_conv_b': nrm(ks[12], (DEPTH, LRU_WIDTH), 0.02),
        'lru_gate_w': nrm(ks[13], (DEPTH, 2, 2, LRU_BLOCKS, LRU_BLOCK, LRU_BLOCK), LRU_BLOCK ** -0.5),
        'lru_gate_b': nrm(ks[15], (DEPTH, 2, 2, LRU_WIDTH), 0.02),
        'lru_lambda': jnp.log(a0) - jnp.log1p(-a0),
        'ssd_conv_w': nrm(ks[16], (DEPTH, CONV_W, SSD_CONV_DIM), CONV_W ** -0.5),
        'ssd_conv_b': nrm(ks[18], (DEPTH, SSD_CONV_DIM), 0.02),
        'ssd_dt_bias': dt0 + jnp.log(-jnp.expm1(-dt0)),
        'ssd_a_log': jnp.log(jax.random.uniform(ks[19], (DEPTH, 2, SSD_HEADS), f32, 1.0, 16.0)),
        'ssd_d': 1.0 + nrm(ks[20], (DEPTH, SSD_HEADS), 0.1),
        'ssd_norm_g': 1.0 + nrm(ks[21], (DEPTH, SSD_INNER), 0.02),
        'w_branch': nrm(ks[22], (DEPTH, N_BRANCH, BRANCH_WIDTH, D), BRANCH_WIDTH ** -0.5),
        'w_out': nrm(ks[23], (DEPTH, D, D), D ** -0.5),
        'peer_wq': nrm(ks[24], (DEPTH, D, PEER_HEADS * PEER_DK), D ** -0.5),
        'peer_keys': nrm(ks[25], (DEPTH, PEER_HEADS, 2, PEER_KEYS, PEER_DK // 2), (PEER_DK // 2) ** -0.5),
        'peer_u': nrm(ks[26], (DEPTH, PEER_N, D), D ** -0.5),
        'peer_v': nrm(ks[27], (DEPTH, PEER_N, D), 1.0),
        'final_g': 1.0 + nrm(ks[28], (D,), 0.02),
    }


def reference(x, c, ctx, c_ctx, ada_w, ada_b, norm1_g, norm2_g, w_in, da_lambda, da_subln_g,
              lru_conv_w, lru_conv_b, lru_gate_w, lru_gate_b, lru_lambda,
              ssd_conv_w, ssd_conv_b, ssd_dt_bias, ssd_a_log, ssd_d, ssd_norm_g,
              w_branch, w_out, peer_wq, peer_keys, peer_u, peer_v, final_g):
    rows = x.shape[1] // GRID_W
    cos, sin = axial_rope(rows, x.dtype)
    cx = ctx
    for l in range(DEPTH):
        need_ctx = l < DEPTH - 1
        mod = jax.nn.silu(c) @ ada_w[l] + ada_b[l]
        mod_c = jax.nn.silu(c_ctx) @ ada_w[l] + ada_b[l]
        sh1, sc1, g1, sh2, sc2, g2 = jnp.split(mod[:, None, :], N_MOD, axis=-1)
        csh1, csc1, cg1, csh2, csc2, cg2 = jnp.split(mod_c, N_MOD, axis=-1)
        h_lat = rmsnorm(x, norm1_g[l]) * (1.0 + sc1) + sh1
        h_ctx = rmsnorm(cx, norm1_g[l]) * (1.0 + csc1) + csh1
        y_lat, y_ctx = token_mixer(h_lat, h_ctx, cos, sin, l, need_ctx, w_in[l], da_lambda[l], da_subln_g[l],
                                   lru_conv_w[l], lru_conv_b[l], lru_gate_w[l], lru_gate_b[l], lru_lambda[l],
                                   ssd_conv_w[l], ssd_conv_b[l], ssd_dt_bias[l], ssd_a_log[l], ssd_d[l], ssd_norm_g[l],
                                   w_branch[l], w_out[l])
        x = x + g1 * y_lat
        x = x + g2 * peer_ffn(rmsnorm(x, norm2_g[l]) * (1.0 + sc2) + sh2, peer_wq[l], peer_keys[l], peer_u[l], peer_v[l])
        if need_ctx:
            cx = cx + cg1 * y_ctx
            cx = cx + cg2 * peer_ffn(rmsnorm(cx, norm2_g[l]) * (1.0 + csc2) + csh2, peer_wq[l], peer_keys[l], peer_u[l], peer_v[l])
    return rmsnorm(x, final_g)
```

```python
import functools
import math

import jax
import jax.numpy as jnp
from jax import lax
from jax.experimental import pallas as pl
from jax.experimental.pallas import tpu as pltpu

F32 = jnp.float32
BF16 = jnp.bfloat16
HIGHEST = lax.Precision.HIGHEST

RMS_EPS = 1e-6
N_MOD = 6
GRID_W = 64
ROPE_BASE = 10000.0

DA_HEADS = 8
DA_HEAD_DIM = 64
HEAD_W = 2 * DA_HEAD_DIM

LRU_WIDTH = 1024
LRU_C = 8.0
CONV_W = 4

SSD_HEADS = 16
SSD_HEAD_DIM = 64
SSD_INNER = SSD_HEADS * SSD_HEAD_DIM
SSD_STATE = 64
SSD_GROUPS = 4
SSD_BC = SSD_GROUPS * SSD_STATE
SSD_CONV_DIM = SSD_INNER + 2 * SSD_BC
SSD_CHUNK = 128

PEER_HEADS = 8
PEER_KEYS = 128
PEER_TOPK = 16
PEER_HALF = 128

LANES = 128
ROW_TILE = 256
SCAN_TILE = 64
CONV_CHUNK = 256
CONV_PAD = 8
NEG_BIG = -1e30
VMEM_LIMIT = 56 * 1024 * 1024


def _cparams(sem, vmem=None):
    return pltpu.CompilerParams(dimension_semantics=sem, vmem_limit_bytes=vmem)


def _mod_row(i, tiles_per_batch, ctx_tiles, n_batch):
    return jnp.where(i % tiles_per_batch < ctx_tiles, n_batch, i // tiles_per_batch)


def _silu(x):
    return x * jax.nn.sigmoid(x)


def _gelu_tanh(x):
    return 0.5 * x * (1.0 + jnp.tanh(math.sqrt(2.0 / math.pi) * (x + 0.044715 * (x * x * x))))


def _softplus(x):
    return jnp.maximum(x, 0.0) + jnp.log1p(jnp.exp(-jnp.abs(x)))


def _modulation_kernel(c_ref, w_ref, b_ref, o_ref):
    s = _silu(c_ref[...]).astype(BF16)
    o_ref[...] = jnp.dot(s, w_ref[...].astype(BF16), preferred_element_type=F32) + b_ref[...]


def modulation(cvec, ada_w, ada_b, tn=1024):
    depth, d, n = ada_w.shape
    return pl.pallas_call(
        _modulation_kernel,
        out_shape=jax.ShapeDtypeStruct((depth, 8, n), F32),
        grid=(depth, n // tn),
        in_specs=[
            pl.BlockSpec((8, d), lambda l, j: (0, 0)),
            pl.BlockSpec((None, d, tn), lambda l, j: (l, 0, j)),
            pl.BlockSpec((None, 1, tn), lambda l, j: (l, 0, j)),
        ],
        out_specs=pl.BlockSpec((None, 8, tn), lambda l, j: (l, 0, j)),
        compiler_params=_cparams(("arbitrary", "arbitrary"), VMEM_LIMIT),
        name="modulation",
    )(cvec, ada_w, ada_b.reshape(depth, 1, n))


def _norm_mod_kernel(x_ref, g_ref, sc_ref, sh_ref, o_ref):
    x = x_ref[...]
    y = x * lax.rsqrt(jnp.mean(x * x, axis=-1, keepdims=True) + RMS_EPS) * g_ref[...]
    o_ref[...] = (y * (1.0 + sc_ref[...]) + sh_ref[...]).astype(o_ref.dtype)


def norm_mod(x, g, modl, sh_idx, sc_idx, geo):
    m, d = x.shape
    tpb, ctx_tiles, nb = geo
    row = lambda i: _mod_row(i, tpb, ctx_tiles, nb)
    return pl.pallas_call(
        _norm_mod_kernel,
        out_shape=jax.ShapeDtypeStruct((m, d), BF16),
        grid=(m // ROW_TILE,),
        in_specs=[
            pl.BlockSpec((ROW_TILE, d), lambda i: (i, 0)),
            pl.BlockSpec((1, d), lambda i: (0, 0)),
            pl.BlockSpec((None, None, 1, d), lambda i: (row(i), sc_idx, 0, 0)),
            pl.BlockSpec((None, None, 1, d), lambda i: (row(i), sh_idx, 0, 0)),
        ],
        out_specs=pl.BlockSpec((ROW_TILE, d), lambda i: (i, 0)),
        compiler_params=_cparams(("arbitrary",)),
        name="norm_mod",
    )(x, g.reshape(1, d), modl, modl)


def _final_norm_kernel(x_ref, g_ref, o_ref):
    x = x_ref[...]
    o_ref[...] = x * lax.rsqrt(jnp.mean(x * x, axis=-1, keepdims=True) + RMS_EPS) * g_ref[...]


def final_norm(x, g, geo, seq_tiles):
    m, d = x.shape
    tpb, ctx_tiles, nb = geo
    return pl.pallas_call(
        _final_norm_kernel,
        out_shape=jax.ShapeDtypeStruct((nb * seq_tiles * ROW_TILE, d), F32),
        grid=(nb, seq_tiles),
        in_specs=[
            pl.BlockSpec((ROW_TILE, d), lambda b, j: (b * tpb + ctx_tiles + j, 0)),
            pl.BlockSpec((1, d), lambda b, j: (0, 0)),
        ],
        out_specs=pl.BlockSpec((ROW_TILE, d), lambda b, j: (b * seq_tiles + j, 0)),
        compiler_params=_cparams(("arbitrary", "arbitrary")),
        name="final_norm",
    )(x, g.reshape(1, d))


def _matmul_kernel(a_ref, b_ref, o_ref):
    o_ref[...] = jnp.dot(a_ref[...], b_ref[...], preferred_element_type=F32).astype(o_ref.dtype)


def matmul(a, b, out_dtype, tm, tn, name):
    m, k = a.shape
    n = b.shape[1]
    return pl.pallas_call(
        _matmul_kernel,
        out_shape=jax.ShapeDtypeStruct((m, n), out_dtype),
        grid=(m // tm, n // tn),
        in_specs=[pl.BlockSpec((tm, k), lambda i, j: (i, 0)), pl.BlockSpec((k, tn), lambda i, j: (0, j))],
        out_specs=pl.BlockSpec((tm, tn), lambda i, j: (i, j)),
        compiler_params=_cparams(("arbitrary", "arbitrary"), VMEM_LIMIT),
        name=name,
    )(a, b)


def _matmul_residual_kernel(a_ref, b_ref, x_ref, g_ref, o_ref):
    y = jnp.dot(a_ref[...], b_ref[...], preferred_element_type=F32)
    o_ref[...] = x_ref[...] + g_ref[...] * y


def matmul_residual(a, b, xres, modl, gate_idx, geo, tn=1024):
    m, k = a.shape
    n = b.shape[1]
    tpb, ctx_tiles, nb = geo
    row = lambda i: _mod_row(i, tpb, ctx_tiles, nb)
    return pl.pallas_call(
        _matmul_residual_kernel,
        out_shape=jax.ShapeDtypeStruct((m, n), F32),
        grid=(m // ROW_TILE, n // tn),
        in_specs=[
            pl.BlockSpec((ROW_TILE, k), lambda i, j: (i, 0)),
            pl.BlockSpec((k, tn), lambda i, j: (0, j)),
            pl.BlockSpec((ROW_TILE, tn), lambda i, j: (i, j)),
            pl.BlockSpec((None, None, 1, tn), lambda i, j: (row(i), gate_idx, 0, j)),
        ],
        out_specs=pl.BlockSpec((ROW_TILE, tn), lambda i, j: (i, j)),
        compiler_params=_cparams(("arbitrary", "arbitrary"), VMEM_LIMIT),
        name="matmul_residual",
    )(a, b, xres, modl)


def _dt_kernel(a_ref, w_ref, wt_ref, b_ref, bt_ref, oc_ref, or_ref):
    a = a_ref[...]
    oc_ref[...] = _softplus(jnp.dot(a, w_ref[...], preferred_element_type=F32) + b_ref[...])
    dtr = lax.dot_general(wt_ref[...], a, (((1,), (1,)), ((), ())), preferred_element_type=F32)
    or_ref[...] = _softplus(dtr + bt_ref[...])


def dt_project(h, w_dt, dt_bias, tm=512):
    m, k = h.shape
    nh = w_dt.shape[1]
    w_pad = jnp.pad(w_dt, ((0, 0), (0, LANES - nh))).astype(BF16)
    b_pad = jnp.pad(dt_bias.reshape(1, nh), ((0, 0), (0, LANES - nh)))
    return pl.pallas_call(
        _dt_kernel,
        out_shape=(jax.ShapeDtypeStruct((m, LANES), F32), jax.ShapeDtypeStruct((nh, m), F32)),
        grid=(m // tm,),
        in_specs=[
            pl.BlockSpec((tm, k), lambda i: (i, 0)),
            pl.BlockSpec((k, LANES), lambda i: (0, 0)),
            pl.BlockSpec((nh, k), lambda i: (0, 0)),
            pl.BlockSpec((1, LANES), lambda i: (0, 0)),
            pl.BlockSpec((nh, 1), lambda i: (0, 0)),
        ],
        out_specs=(pl.BlockSpec((tm, LANES), lambda i: (i, 0)), pl.BlockSpec((nh, tm), lambda i: (0, i))),
        compiler_params=_cparams(("arbitrary",)),
        name="dt_project",
    )(h, w_pad, w_dt.T.astype(BF16), b_pad, dt_bias.reshape(nh, 1))


def rope_tables(rows, ctx_len):
    row = jnp.repeat(jnp.arange(rows), GRID_W).astype(F32)
    col = jnp.tile(jnp.arange(GRID_W), rows).astype(F32)
    n_freq = DA_HEAD_DIM // 4
    inv = jnp.power(ROPE_BASE, -jnp.arange(n_freq, dtype=F32) / n_freq)
    ang = jnp.concatenate([row[:, None] * inv, col[:, None] * inv], axis=-1)
    cos, sin = jnp.cos(ang), jnp.sin(ang)
    cos_t = jnp.concatenate([cos, cos, cos, cos], axis=-1)
    sin_t = jnp.concatenate([-sin, sin, -sin, sin], axis=-1)
    cos_t = jnp.concatenate([jnp.ones((ctx_len, HEAD_W), F32), cos_t], axis=0)
    sin_t = jnp.concatenate([jnp.zeros((ctx_len, HEAD_W), F32), sin_t], axis=0)
    return cos_t, sin_t


def _qkv_prep_kernel(u_ref, cos_ref, sin_ref, o_ref, *, n_heads):
    j = pl.program_id(1)
    t = u_ref[...]

    @pl.when(j < 2 * n_heads)
    def _():
        lane = lax.broadcasted_iota(jnp.int32, t.shape, 1)
        first = (lane % DA_HEAD_DIM) < DA_HEAD_DIM // 2
        half = DA_HEAD_DIM // 2
        swapped = jnp.where(first, pltpu.roll(t, HEAD_W - half, 1), pltpu.roll(t, half, 1))
        r = t * cos_ref[...] + swapped * sin_ref[...]
        scale = jnp.where(j < n_heads, DA_HEAD_DIM ** -0.5, 1.0)
        o_ref[...] = (r * scale).astype(o_ref.dtype)

    @pl.when(j >= 2 * n_heads)
    def _():
        o_ref[...] = t.astype(o_ref.dtype)


def qkv_prep(u_main, cos_t, sin_t, rows_per_batch, tm):
    m = u_main.shape[0]
    n_blocks = 3 * DA_HEADS
    tiles = rows_per_batch // tm
    return pl.pallas_call(
        functools.partial(_qkv_prep_kernel, n_heads=DA_HEADS),
        out_shape=jax.ShapeDtypeStruct((m, n_blocks * HEAD_W), BF16),
        grid=(m // tm, n_blocks),
        in_specs=[
            pl.BlockSpec((tm, HEAD_W), lambda i, j: (i, j)),
            pl.BlockSpec((tm, HEAD_W), lambda i, j: (i % tiles, 0)),
            pl.BlockSpec((tm, HEAD_W), lambda i, j: (i % tiles, 0)),
        ],
        out_specs=pl.BlockSpec((tm, HEAD_W), lambda i, j: (i, j)),
        compiler_params=_cparams(("arbitrary", "arbitrary")),
        name="qkv_prep",
    )(u_main, cos_t, sin_t)


def _attn_kernel(lp_ref, g_ref, q_ref, k_ref, v_ref, o_ref, *, tk, ctx_q_tiles, ctx_chunks, n_chunks, lam_init):
    qi = pl.program_id(2)
    q = q_ref[...]
    tq = q.shape[0]
    lane = lax.broadcasted_iota(jnp.int32, q.shape, 1)
    zero = jnp.zeros_like(q)
    q_parts = (jnp.where(lane < DA_HEAD_DIM, q, zero), jnp.where(lane >= DA_HEAD_DIM, q, zero))

    def body(j, carry):
        off = pl.multiple_of(j * tk, tk)
        k = k_ref[pl.ds(off, tk), :]
        v = v_ref[pl.ds(off, tk), :]
        out = []
        for c in range(2):
            m, l, acc = carry[3 * c : 3 * c + 3]
            s = lax.dot_general(q_parts[c], k, (((1,), (1,)), ((), ())), preferred_element_type=F32)
            m_new = jnp.maximum(m, jnp.max(s, axis=1, keepdims=True))
            alpha = jnp.exp(m - m_new)
            p = jnp.exp(s - m_new)
            l = alpha * l + jnp.sum(p, axis=1, keepdims=True)
            acc = alpha * acc + jnp.dot(p.astype(BF16), v, preferred_element_type=F32)
            out += [m_new, l, acc]
        return tuple(out)

    one = (jnp.full((tq, 1), NEG_BIG, F32), jnp.zeros((tq, 1), F32), jnp.zeros((tq, HEAD_W), F32))
    n = jnp.where(qi < ctx_q_tiles, ctx_chunks, n_chunks)
    m1, l1, a1, m2, l2, a2 = lax.fori_loop(0, n, body, one + one)

    lp = lp_ref[...]
    lam = (
        jnp.exp(jnp.sum(lp[0:1] * lp[1:2], axis=1, keepdims=True))
        - jnp.exp(jnp.sum(lp[2:3] * lp[3:4], axis=1, keepdims=True))
        + lam_init
    )
    o = a1 / l1 - lam * (a2 / l2)
    y = o * lax.rsqrt(jnp.mean(o * o, axis=-1, keepdims=True) + RMS_EPS) * g_ref[...]
    o_ref[...] = (y * (1.0 - lam_init)).astype(o_ref.dtype)


def diff_attention(qkv, da_lambda, subln_g, layer, n_batch, rows, ctx_len, tq=ROW_TILE, tk=ROW_TILE):
    qkv3 = qkv.reshape(n_batch, rows, 3 * DA_HEADS * HEAD_W)
    lam_init = 0.8 - 0.6 * math.exp(-0.3 * layer)
    kern = functools.partial(
        _attn_kernel, tk=tk, ctx_q_tiles=ctx_len // tq, ctx_chunks=ctx_len // tk, n_chunks=rows // tk, lam_init=lam_init
    )
    out = pl.pallas_call(
        kern,
        out_shape=jax.ShapeDtypeStruct((n_batch, rows, DA_HEADS * HEAD_W), BF16),
        grid=(n_batch, DA_HEADS, rows // tq),
        in_specs=[
            pl.BlockSpec((4, DA_HEAD_DIM), lambda b, h, i: (0, 0)),
            pl.BlockSpec((1, HEAD_W), lambda b, h, i: (0, 0)),
            pl.BlockSpec((None, tq, HEAD_W), lambda b, h, i: (b, i, h)),
            pl.BlockSpec((None, rows, HEAD_W), lambda b, h, i: (b, 0, DA_HEADS + h)),
            pl.BlockSpec((None, rows, HEAD_W), lambda b, h, i: (b, 0, 2 * DA_HEADS + h)),
        ],
        out_specs=pl.BlockSpec((None, tq, HEAD_W), lambda b, h, i: (b, i, h)),
        compiler_params=_cparams(("arbitrary", "arbitrary", "arbitrary"), VMEM_LIMIT),
        name="diff_attention",
    )(da_lambda, subln_g.reshape(1, HEAD_W), qkv3, qkv3, qkv3)
    return out.reshape(n_batch * rows, DA_HEADS * HEAD_W)


def _fill_padded(xp_ref, x_ref, rows):
    zeros = jnp.zeros((CONV_PAD, xp_ref.shape[1]), F32)
    xp_ref[pl.ds(0, CONV_PAD), :] = zeros
    xp_ref[pl.ds(CONV_PAD, rows), :] = x_ref[...]
    xp_ref[pl.ds(CONV_PAD + rows, CONV_PAD), :] = zeros


def _conv_chunk(xp_ref, r0, w_ref, b_ref, rows, ctx_len):
    n = CONV_CHUNK
    v = xp_ref[pl.ds(r0, n + 2 * CONV_PAD), :]
    row = r0 + lax.broadcasted_iota(jnp.int32, (n, v.shape[1]), 0)
    in_ctx = row < ctx_len
    pos = jnp.where(in_ctx, row, row - ctx_len)
    last = jnp.where(in_ctx, ctx_len, rows - ctx_len) - 1
    total = n + 2 * CONV_PAD

    def shifted(k):
        return pltpu.roll(v, (-k) % total, 0)[CONV_PAD : CONV_PAD + n]

    w = w_ref[...]
    y = w[2:3] * v[CONV_PAD : CONV_PAD + n] + b_ref[...]
    y = y + w[0:1] * jnp.where(pos >= 2, shifted(-2), 0.0)
    y = y + w[1:2] * jnp.where(pos >= 1, shifted(-1), 0.0)
    y = y + w[3:4] * jnp.where(pos < last, shifted(1), 0.0)
    return y


def _lru_kernel(x_ref, y_ref, cw_ref, cb_ref, gw_ref, gb_ref, lam_ref, o_ref, xp, a_f, b_f, a_b, b_b, *, rows, ctx_len):
    _fill_padded(xp, x_ref, rows)
    sp = _softplus(-lam_ref[...])
    gw = gw_ref[...]
    gb = gb_ref[...]

    def coeff_body(i, _):
        r0 = pl.multiple_of(i * CONV_CHUNK, CONV_CHUNK)
        xc = _conv_chunk(xp, r0, cw_ref, cb_ref, rows, ctx_len)
        gates = jnp.dot(xc.astype(BF16), gw, preferred_element_type=F32) + gb
        for d, (a_ref, b_ref) in enumerate(((a_f, b_f), (a_b, b_b))):
            r = jax.nn.sigmoid(gates[:, (2 * d) * LANES : (2 * d + 1) * LANES])
            ig = jax.nn.sigmoid(gates[:, (2 * d + 1) * LANES : (2 * d + 2) * LANES])
            log_a = -LRU_C * r * sp[d : d + 1]
            a = jnp.exp(log_a)
            a_ref[pl.ds(r0, CONV_CHUNK), :] = a
            b_ref[pl.ds(r0, CONV_CHUNK), :] = jnp.sqrt(-jnp.tanh(log_a) * (a * a + 1.0)) * (ig * xc)
        return 0

    lax.fori_loop(0, rows // CONV_CHUNK, coeff_body, 0)

    row8 = lax.broadcasted_iota(jnp.int32, (SCAN_TILE, LANES), 0) % 8
    n_sub = SCAN_TILE // 8

    def fwd_body(t, h):
        r0 = pl.multiple_of(t * SCAN_TILE, SCAN_TILE)
        a = a_f[pl.ds(r0, SCAN_TILE), :]
        b = b_f[pl.ds(r0, SCAN_TILE), :]
        for s in (1, 2, 4):
            keep = row8 >= s
            a_s = jnp.where(keep, pltpu.roll(a, s, 0), 1.0)
            b_s = jnp.where(keep, pltpu.roll(b, s, 0), 0.0)
            b = a * b_s + b
            a = a * a_s
        outs = []
        for v in range(n_sub):
            hv = a[8 * v : 8 * v + 8] * h + b[8 * v : 8 * v + 8]
            outs.append(hv)
            h = hv[7:8]
        b_f[pl.ds(r0, SCAN_TILE), :] = jnp.concatenate(outs, axis=0)
        return h

    lax.fori_loop(0, rows // SCAN_TILE, fwd_body, jnp.zeros((1, LANES), F32))

    ctx_tiles = ctx_len // SCAN_TILE
    n_tiles = rows // SCAN_TILE

    def bwd_body(k, h):
        t = jnp.where(k < ctx_tiles, ctx_tiles - 1 - k, n_tiles - 1 + ctx_tiles - k)
        r0 = pl.multiple_of(t * SCAN_TILE, SCAN_TILE)
        a = a_b[pl.ds(r0, SCAN_TILE), :]
        b = b_b[pl.ds(r0, SCAN_TILE), :]
        for s in (1, 2, 4):
            keep = row8 < 8 - s
            a_s = jnp.where(keep, pltpu.roll(a, SCAN_TILE - s, 0), 1.0)
            b_s = jnp.where(keep, pltpu.roll(b, SCAN_TILE - s, 0), 0.0)
            b = a * b_s + b
            a = a * a_s
        outs = [None] * n_sub
        for v in reversed(range(n_sub)):
            hv = a[8 * v : 8 * v + 8] * h + b[8 * v : 8 * v + 8]
            outs[v] = hv
            h = hv[0:1]
        hb = jnp.concatenate(outs, axis=0)
        hf = b_f[pl.ds(r0, SCAN_TILE), :]
        o_ref[pl.ds(r0, SCAN_TILE), :] = ((hf + hb) * _gelu_tanh(y_ref[pl.ds(r0, SCAN_TILE), :])).astype(o_ref.dtype)
        return h

    lax.fori_loop(0, n_tiles, bwd_body, jnp.zeros((1, LANES), F32))


def lru_branch(u_main, x_col0, y_col0, conv_w, conv_b, gate_w, gate_b, lam, n_batch, rows, ctx_len):
    width = conv_w.shape[1]
    n_blk = width // LANES
    u3 = u_main.reshape(n_batch, rows, u_main.shape[1])
    gw = jnp.transpose(gate_w, (2, 3, 0, 1, 4)).reshape(n_blk, LANES, 4 * LANES).astype(BF16)
    gb = jnp.transpose(gate_b.reshape(2, 2, n_blk, LANES), (2, 0, 1, 3)).reshape(n_blk, 1, 4 * LANES)
    seq_block = lambda c0: pl.BlockSpec((None, rows, LANES), lambda b, g: (b, 0, c0 + g))
    scratch = [pltpu.VMEM((rows + 2 * CONV_PAD, LANES), F32)] + [pltpu.VMEM((rows, LANES), F32)] * 4
    out = pl.pallas_call(
        functools.partial(_lru_kernel, rows=rows, ctx_len=ctx_len),
        out_shape=jax.ShapeDtypeStruct((n_batch, rows, width), BF16),
        grid=(n_batch, n_blk),
        in_specs=[
            seq_block(x_col0),
            seq_block(y_col0),
            pl.BlockSpec((CONV_W, LANES), lambda b, g: (0, g)),
            pl.BlockSpec((1, LANES), lambda b, g: (0, g)),
            pl.BlockSpec((None, LANES, 4 * LANES), lambda b, g: (g, 0, 0)),
            pl.BlockSpec((None, 1, 4 * LANES), lambda b, g: (g, 0, 0)),
            pl.BlockSpec((2, LANES), lambda b, g: (0, g)),
        ],
        out_specs=pl.BlockSpec((None, rows, LANES), lambda b, g: (b, 0, g)),
        scratch_shapes=scratch,
        compiler_params=_cparams(("arbitrary", "arbitrary"), VMEM_LIMIT),
        name="lru_branch",
    )(u3, u3, conv_w, conv_b.reshape(1, width), gw, gb, lam)
    return out.reshape(n_batch * rows, width)


def _ssd_conv_kernel(x_ref, cw_ref, cb_ref, o_ref, xp, *, rows, ctx_len):
    _fill_padded(xp, x_ref, rows)

    def body(i, _):
        r0 = pl.multiple_of(i * CONV_CHUNK, CONV_CHUNK)
        o_ref[pl.ds(r0, CONV_CHUNK), :] = _silu(_conv_chunk(xp, r0, cw_ref, cb_ref, rows, ctx_len))
        return 0

    lax.fori_loop(0, rows // CONV_CHUNK, body, 0)


def ssd_conv(u_main, col0, conv_w, conv_b, n_batch, rows, ctx_len):
    width = conv_w.shape[1]
    u3 = u_main.reshape(n_batch, rows, u_main.shape[1])
    return pl.pallas_call(
        functools.partial(_ssd_conv_kernel, rows=rows, ctx_len=ctx_len),
        out_shape=jax.ShapeDtypeStruct((n_batch, rows, width), F32),
        grid=(n_batch, width // LANES),
        in_specs=[
            pl.BlockSpec((None, rows, LANES), lambda b, g: (b, 0, col0 + g)),
            pl.BlockSpec((CONV_W, LANES), lambda b, g: (0, g)),
            pl.BlockSpec((1, LANES), lambda b, g: (0, g)),
        ],
        out_specs=pl.BlockSpec((None, rows, LANES), lambda b, g: (b, 0, g)),
        scratch_shapes=[pltpu.VMEM((rows + 2 * CONV_PAD, LANES), F32)],
        compiler_params=_cparams(("arbitrary", "arbitrary"), VMEM_LIMIT),
        name="ssd_conv",
    )(u3, conv_w, conv_b.reshape(1, width))


def _ssd_chunk_of_step(s, nc, ctx_chunks):
    back = jnp.where(s < ctx_chunks, ctx_chunks - 1 - s, nc - 1 + ctx_chunks - s)
    return jnp.where(s < nc, back, s - nc)


def _ssd_kernel(xbc_ref, z_ref, dtc_ref, dtr_ref, alr_ref, alc_ref, dsk_ref, ng_ref, o_ref, s_f, s_b, y_b, *, nc, ctx_chunks):
    step = pl.program_id(1)
    c = _ssd_chunk_of_step(step, nc, ctx_chunks)
    L = SSD_CHUNK
    nh = SSD_HEADS
    hp = SSD_HEAD_DIM
    gw = SSD_INNER // SSD_GROUPS

    @pl.when(step == 0)
    def _():
        s_b[...] = jnp.zeros_like(s_b)

    @pl.when(step == nc)
    def _():
        s_f[...] = jnp.zeros_like(s_f)

    xbc = xbc_ref[...]
    x = xbc[:, :SSD_INNER]
    bm = xbc[:, SSD_INNER : SSD_INNER + SSD_BC]
    cm = xbc[:, SSD_INNER + SSD_BC :]
    dtc = dtc_ref[...]
    dtr = dtr_ref[...]
    da_c = dtc * -jnp.exp(alr_ref[...])
    da_r = dtr * -jnp.exp(alc_ref[...])
    li = lax.broadcasted_iota(jnp.int32, (L, L), 0)
    si = lax.broadcasted_iota(jnp.int32, (L, L), 1)
    lower = li >= si
    upper = li <= si
    lower_f = lower.astype(F32)
    upper_f = upper.astype(F32)
    hdot = functools.partial(jnp.dot, precision=HIGHEST, preferred_element_type=F32)
    ej = lax.broadcasted_iota(jnp.int32, (LANES, SSD_INNER), 0)
    en = lax.broadcasted_iota(jnp.int32, (LANES, SSD_INNER), 1) // hp
    sr = lax.broadcasted_iota(jnp.int32, (SSD_BC, SSD_INNER), 0) // SSD_STATE
    sc = lax.broadcasted_iota(jnp.int32, (SSD_BC, SSD_INNER), 1) // gw
    bmt = bm.T.astype(BF16)

    def expand(vals, e):
        hi = vals.astype(BF16)
        lo = (vals - hi.astype(F32)).astype(BF16)
        return jnp.dot(hi, e, preferred_element_type=F32) + jnp.dot(lo, e, preferred_element_type=F32)

    def sweep(cum_c, tot, e, state_ref):
        scale = expand(jnp.exp(cum_c), e)
        state = state_ref[...]
        y_off = scale * jnp.dot(cm.astype(BF16), state.astype(BF16), preferred_element_type=F32)
        w = expand(jnp.exp(tot - cum_c) * dtc, e)
        upd = jnp.dot(bmt, (w * x).astype(BF16), preferred_element_type=F32)
        decay = expand(jnp.broadcast_to(jnp.exp(tot), (8, LANES)), e)[0:1]
        state_ref[...] = decay * state + jnp.where(sr == sc, upd, 0.0)
        return y_off

    @pl.when(step < nc)
    def _():
        rcum_c = hdot(upper_f, da_c)
        e_b = (ej == en + nh).astype(BF16)
        y_b[c] = sweep(rcum_c, rcum_c[0:1], e_b, s_b)

    @pl.when(step >= nc)
    def _():
        cum_c = hdot(lower_f, da_c)
        rcum_c = hdot(upper_f, da_c)
        cum_r = hdot(da_r, upper_f)
        rcum_r = hdot(da_r, lower_f)
        e_f = (ej == en).astype(BF16)
        y_off = sweep(cum_c, cum_c[L - 1 : L], e_f, s_f)
        cmb = cm.astype(BF16)
        bmb = bm.astype(BF16)
        glane = lax.broadcasted_iota(jnp.int32, (L, SSD_BC), 1) // SSD_STATE
        hlane = lax.broadcasted_iota(jnp.int32, (L, gw), 1) // hp
        ys = []
        for g in range(SSD_GROUPS):
            cg = jnp.where(glane == g, cmb, jnp.zeros_like(cmb))
            cb = lax.dot_general(cg, bmb, (((1,), (1,)), ((), ())), preferred_element_type=F32)
            ms = []
            for hh in range(nh // SSD_GROUPS):
                h = g * (nh // SSD_GROUPS) + hh
                seg_f = jnp.where(lower, cum_c[:, h : h + 1] - cum_r[h : h + 1, :], NEG_BIG)
                seg_b = jnp.where(upper, rcum_c[:, nh + h : nh + h + 1] - rcum_r[nh + h : nh + h + 1, :], NEG_BIG)
                gmat = jnp.exp(seg_f) * dtr[h : h + 1, :] + jnp.exp(seg_b) * dtr[nh + h : nh + h + 1, :]
                ms.append((cb * gmat).astype(BF16))
            prod = jnp.dot(jnp.concatenate(ms, axis=0), x[:, g * gw : (g + 1) * gw].astype(BF16), preferred_element_type=F32)
            yg = jnp.zeros((L, gw), F32)
            for hh in range(nh // SSD_GROUPS):
                yg = jnp.where(hlane == hh, prod[hh * L : (hh + 1) * L], yg)
            ys.append(yg)
        y = jnp.concatenate(ys, axis=1) + y_off + y_b[c] + dsk_ref[...] * x
        gated = y * _silu(z_ref[...])
        o = gated * lax.rsqrt(jnp.mean(gated * gated, axis=-1, keepdims=True) + RMS_EPS) * ng_ref[...]
        o_ref[...] = o.astype(o_ref.dtype)


def ssd_branch(xbc_c, u_main, z_col_blk, dt_c, dt_r, a_log, d_skip, norm_g, n_batch, rows, ctx_len):
    nc = rows // SSD_CHUNK
    ctx_chunks = ctx_len // SSD_CHUNK
    u3 = u_main.reshape(n_batch, rows, u_main.shape[1])
    dtc3 = dt_c.reshape(n_batch, rows, LANES)
    nh2 = 2 * SSD_HEADS
    chunk = lambda s: _ssd_chunk_of_step(s, nc, ctx_chunks)
    fwd_chunk = lambda s: jnp.where(s < nc, 0, s - nc)
    alr = jnp.pad(a_log.reshape(1, nh2), ((0, 0), (0, LANES - nh2)))
    dsk = jnp.repeat(d_skip, SSD_HEAD_DIM).reshape(1, SSD_INNER)
    out = pl.pallas_call(
        functools.partial(_ssd_kernel, nc=nc, ctx_chunks=ctx_chunks),
        out_shape=jax.ShapeDtypeStruct((n_batch, rows, SSD_INNER), BF16),
        grid=(n_batch, 2 * nc),
        in_specs=[
            pl.BlockSpec((None, SSD_CHUNK, SSD_CONV_DIM), lambda b, s: (b, chunk(s), 0)),
            pl.BlockSpec((None, SSD_CHUNK, SSD_INNER), lambda b, s: (b, fwd_chunk(s), z_col_blk)),
            pl.BlockSpec((None, SSD_CHUNK, LANES), lambda b, s: (b, chunk(s), 0)),
            pl.BlockSpec((nh2, SSD_CHUNK), lambda b, s: (0, b * nc + chunk(s))),
            pl.BlockSpec((1, LANES), lambda b, s: (0, 0)),
            pl.BlockSpec((nh2, 1), lambda b, s: (0, 0)),
            pl.BlockSpec((1, SSD_INNER), lambda b, s: (0, 0)),
            pl.BlockSpec((1, SSD_INNER), lambda b, s: (0, 0)),
        ],
        out_specs=pl.BlockSpec((None, SSD_CHUNK, SSD_INNER), lambda b, s: (b, fwd_chunk(s), 0)),
        scratch_shapes=[
            pltpu.VMEM((SSD_BC, SSD_INNER), F32),
            pltpu.VMEM((SSD_BC, SSD_INNER), F32),
            pltpu.VMEM((nc, SSD_CHUNK, SSD_INNER), F32),
        ],
        compiler_params=_cparams(("arbitrary", "arbitrary"), VMEM_LIMIT),
        name="ssd_branch",
    )(xbc_c, u3, dtc3, dt_r, alr, a_log.reshape(nh2, 1), dsk, norm_g.reshape(1, SSD_INNER))
    return out.reshape(n_batch * rows, SSD_INNER)


def _merge_kernel(a0_ref, a1_ref, a2_ref, g0_ref, g1_ref, g2_ref, w0_ref, w1_ref, w2_ref, o_ref):
    acc = None
    for a_ref, g_ref, w_ref in ((a0_ref, g0_ref, w0_ref), (a1_ref, g1_ref, w1_ref), (a2_ref, g2_ref, w2_ref)):
        t = jax.nn.sigmoid(g_ref[...]) * jnp.dot(a_ref[...], w_ref[...], preferred_element_type=F32)
        acc = t if acc is None else acc + t
    o_ref[...] = acc.astype(o_ref.dtype)


def merge_branches(att, lru, ssd, gates_u, w_branch, tm=512, tn=512):
    m, k = att.shape
    d = w_branch.shape[2]
    nj = d // tn
    a_spec = pl.BlockSpec((tm, k), lambda i, j: (i, 0))
    g_spec = lambda br: pl.BlockSpec((tm, tn), lambda i, j: (i, br * nj + j))
    w_spec = lambda br: pl.BlockSpec((None, k, tn), lambda i, j: (br, 0, j))
    return pl.pallas_call(
        _merge_kernel,
        out_shape=jax.ShapeDtypeStruct((m, d), BF16),
        grid=(m // tm, nj),
        in_specs=[a_spec, a_spec, a_spec, g_spec(0), g_spec(1), g_spec(2), w_spec(0), w_spec(1), w_spec(2)],
        out_specs=pl.BlockSpec((tm, tn), lambda i, j: (i, j)),
        compiler_params=_cparams(("arbitrary", "arbitrary"), VMEM_LIMIT),
        name="merge_branches",
    )(att, lru, ssd, gates_u, gates_u, gates_u, w_branch, w_branch, w_branch)


def _top_values(s, k):
    iota = lax.broadcasted_iota(jnp.int32, s.shape, 0)
    vals = []
    for _ in range(k):
        m = jnp.max(s, axis=0, keepdims=True)
        first = jnp.min(jnp.where(s == m, iota, s.shape[0]), axis=0, keepdims=True)
        s = jnp.where(iota == first, -jnp.inf, s)
        vals.append(m)
    return vals


def _peer_route_kernel(q_ref, keys_ref, s_ref, e_ref, tau_ref):
    q = q_ref[...]
    keys = keys_ref[...]
    nt = (((1,), (1,)), ((), ()))
    s1 = lax.dot_general(keys[0], q[:, :PEER_HALF], nt, preferred_element_type=F32)
    s2 = lax.dot_general(keys[1], q[:, PEER_HALF:], nt, preferred_element_type=F32)
    t1 = _top_values(s1, PEER_TOPK)
    t2 = jnp.concatenate(_top_values(s2, PEER_TOPK), axis=0)
    cand = jnp.concatenate([t + t2 for t in t1], axis=0)
    best = _top_values(cand, PEER_TOPK)
    z = jnp.zeros_like(best[0])
    for b in best:
        z = z + jnp.exp(b - best[0])
    s_ref[0] = s1
    s_ref[1] = s2
    e_ref[0] = jnp.exp(s1 - t1[0]) / z
    e_ref[1] = jnp.exp(s2 - t2[0:1])
    tau_ref[...] = best[-1]


def peer_route(q, keys, tm=256):
    m = q.shape[0]
    out_sd = jax.ShapeDtypeStruct((PEER_HEADS, 2, PEER_KEYS, m), F32)
    big = pl.BlockSpec((None, 2, PEER_KEYS, tm), lambda i, h: (h, 0, 0, i))
    return pl.pallas_call(
        _peer_route_kernel,
        out_shape=(out_sd, out_sd, jax.ShapeDtypeStruct((PEER_HEADS, 1, m), F32)),
        grid=(m // tm, PEER_HEADS),
        in_specs=[
            pl.BlockSpec((tm, 2 * PEER_HALF), lambda i, h: (i, h)),
            pl.BlockSpec((None, 2, PEER_KEYS, PEER_HALF), lambda i, h: (h, 0, 0, 0)),
        ],
        out_specs=(big, big, pl.BlockSpec((None, 1, tm), lambda i, h: (h, 0, i))),
        compiler_params=_cparams(("arbitrary", "arbitrary"), VMEM_LIMIT),
        name="peer_route",
    )(q, keys.astype(BF16))


def _peer_expert_kernel(h_ref, u_ref, v_ref, s_ref, e_ref, tau_ref, x_ref, g_ref, o_ref, acc, *, te):
    j = pl.program_id(1)

    @pl.when(j == 0)
    def _():
        acc[...] = jnp.zeros_like(acc)

    blocks = []
    for a in range(te // PEER_KEYS):
        i1 = j * (te // PEER_KEYS) + a
        w = None
        for h in range(PEER_HEADS):
            s1 = s_ref[h, 0, pl.ds(i1, 1), :]
            e1 = e_ref[h, 0, pl.ds(i1, 1), :]
            sel = (s1 + s_ref[h, 1]) >= tau_ref[h]
            t = jnp.where(sel, e1 * e_ref[h, 1], 0.0)
            w = t if w is None else w + t
        blocks.append(w)
    w_t = jnp.concatenate(blocks, axis=0)
    act = lax.dot_general(h_ref[...], u_ref[...], (((1,), (1,)), ((), ())), preferred_element_type=F32)
    wa = (_gelu_tanh(act) * w_t.T).astype(BF16)
    acc[...] += jnp.dot(wa, v_ref[...], preferred_element_type=F32)

    @pl.when(j == pl.num_programs(1) - 1)
    def _():
        o_ref[...] = x_ref[...] + g_ref[...] * acc[...]


def peer_experts(hq, u_tab, v_tab, s, e, tau, xres, modl, gate_idx, geo, te=512):
    m, d = hq.shape
    n_exp = u_tab.shape[0]
    tm = ROW_TILE
    tpb, ctx_tiles, nb = geo
    row = lambda i: _mod_row(i, tpb, ctx_tiles, nb)
    route = pl.BlockSpec((PEER_HEADS, 2, PEER_KEYS, tm), lambda i, j: (0, 0, 0, i))
    return pl.pallas_call(
        functools.partial(_peer_expert_kernel, te=te),
        out_shape=jax.ShapeDtypeStruct((m, d), F32),
        grid=(m // tm, n_exp // te),
        in_specs=[
            pl.BlockSpec((tm, d), lambda i, j: (i, 0)),
            pl.BlockSpec((te, d), lambda i, j: (j, 0)),
            pl.BlockSpec((te, d), lambda i, j: (j, 0)),
            route,
            route,
            pl.BlockSpec((PEER_HEADS, 1, tm), lambda i, j: (0, 0, i)),
            pl.BlockSpec((tm, d), lambda i, j: (i, 0)),
            pl.BlockSpec((None, None, 1, d), lambda i, j: (row(i), gate_idx, 0, 0)),
        ],
        out_specs=pl.BlockSpec((tm, d), lambda i, j: (i, 0)),
        scratch_shapes=[pltpu.VMEM((tm, d), F32)],
        compiler_params=_cparams(("arbitrary", "arbitrary"), VMEM_LIMIT),
        name="peer_experts",
    )(hq, u_tab, v_tab, s, e, tau, xres, modl)


def kernel(x, c, ctx, c_ctx, ada_w, ada_b, norm1_g, norm2_g, w_in, da_lambda, da_subln_g, lru_conv_w, lru_conv_b, lru_gate_w, lru_gate_b, lru_lambda, ssd_conv_w, ssd_conv_b, ssd_dt_bias, ssd_a_log, ssd_d, ssd_norm_g, w_branch, w_out, peer_wq, peer_keys, peer_u, peer_v, final_g):
    n_batch, seq, d = x.shape
    ctx_len = ctx.shape[1]
    depth = ada_w.shape[0]
    rows = ctx_len + seq
    m = n_batch * rows
    assert ctx_len % ROW_TILE == 0 and seq % ROW_TILE == 0 and seq % GRID_W == 0 and n_batch < 8
    geo = (rows // ROW_TILE, ctx_len // ROW_TILE, n_batch)

    cvec = jnp.concatenate([c, c_ctx[None], jnp.zeros((8 - n_batch - 1, d), F32)], axis=0)
    mod = modulation(cvec, ada_w, ada_b)
    cos_t, sin_t = rope_tables(seq // GRID_W, ctx_len)
    xs = jnp.concatenate([ctx, x], axis=1).reshape(m, d)

    qk_w = DA_HEADS * HEAD_W
    main_w = 3 * qk_w + 2 * LRU_WIDTH + SSD_INNER + SSD_CONV_DIM
    dt_w = 2 * SSD_HEADS
    big_tm = m // 8 if (m // 8) % 16 == 0 else ROW_TILE
    prep_tm = rows // 4 if (rows // 4) % 16 == 0 else ROW_TILE

    for l in range(depth):
        modl = mod[l].reshape(8, N_MOD, 1, d)
        h = norm_mod(xs, norm1_g[l], modl, 0, 1, geo)
        w_l = w_in[l]
        u_main = matmul(h, w_l[:, :main_w].astype(BF16), F32, big_tm, 512, "in_proj")
        gates_u = matmul(h, w_l[:, main_w + dt_w :].astype(BF16), F32, big_tm, 512, "gate_proj")
        dt_c, dt_r = dt_project(h, w_l[:, main_w : main_w + dt_w], ssd_dt_bias[l].reshape(dt_w))

        qkv = qkv_prep(u_main, cos_t, sin_t, rows, prep_tm)
        att = diff_attention(qkv, da_lambda[l], da_subln_g[l], l, n_batch, rows, ctx_len)
        lru = lru_branch(u_main, 3 * qk_w // LANES, (3 * qk_w + LRU_WIDTH) // LANES, lru_conv_w[l], lru_conv_b[l],
                         lru_gate_w[l], lru_gate_b[l], lru_lambda[l], n_batch, rows, ctx_len)
        xbc_col0 = (3 * qk_w + 2 * LRU_WIDTH + SSD_INNER) // LANES
        xbc_c = ssd_conv(u_main, xbc_col0, ssd_conv_w[l], ssd_conv_b[l], n_batch, rows, ctx_len)
        z_blk = (3 * qk_w + 2 * LRU_WIDTH) // SSD_INNER
        ssd = ssd_branch(xbc_c, u_main, z_blk, dt_c, dt_r, ssd_a_log[l], ssd_d[l], ssd_norm_g[l], n_batch, rows, ctx_len)

        mixed = merge_branches(att, lru, ssd, gates_u, w_branch[l].astype(BF16))
        xs = matmul_residual(mixed, w_out[l].astype(BF16), xs, modl, 2, geo)

        hq = norm_mod(xs, norm2_g[l], modl, 3, 4, geo)
        q = matmul(hq, peer_wq[l].astype(BF16), BF16, big_tm, 512, "peer_query")
        s, e, tau = peer_route(q, peer_keys[l])
        xs = peer_experts(hq, peer_u[l].astype(BF16), peer_v[l].astype(BF16), s, e, tau, xs, modl, 5, geo)

    out = final_norm(xs, final_g, geo, seq // ROW_TILE)
    return out.reshape(n_batch, seq, d)
```
